```python
import jax, jax.numpy as jnp
from jax import lax
import numpy as np

D_MODEL = 2048
BATCH = 2
SEQ = 16384
DEPTH = 1

HEAD_DIM = 128
A_Q_HEADS = 8
A_KV_HEADS = 2
B_HEADS = 8
DIL_WINDOWS = (128, 512, 2048)
DIL_RATES = (1, 4, 16)
N_DIL = 3
MEM_TOKENS = 256
MEM_HEADS = 4
D_FF = 5632
NUM_BUCKETS = 32
MAX_DISTANCE = 1024
GRID_W = 64
ROPE_THETA = 10000.0
QBLOCK = 128
EPS = 1e-6
NEG = -1e30

A_Q_W = A_Q_HEADS * HEAD_DIM
A_KV_W = A_KV_HEADS * HEAD_DIM
B_W = B_HEADS * HEAD_DIM
IN_W = A_Q_W + 2 * A_KV_W + 3 * N_DIL * B_W
MIX_W = A_Q_W + B_W
MEM_W = MEM_HEADS * HEAD_DIM

kernel_name = "hybrid_gqa_axial_dilated_memory_macaron"


def rms_norm(x, g):
    xf = x.astype(jnp.float32)
    y = xf * lax.rsqrt(jnp.mean(xf * xf, axis=-1, keepdims=True) + EPS)
    return (y * g.astype(jnp.float32)).astype(x.dtype)


def swiglu(h, w_gate, w_up, w_down):
    return (jax.nn.silu(h @ w_gate) * (h @ w_up)) @ w_down


def axial_rope_tables(seq_len):
    rows = seq_len // GRID_W
    row = jnp.repeat(jnp.arange(rows), GRID_W).astype(jnp.float32)
    col = jnp.tile(jnp.arange(GRID_W), rows).astype(jnp.float32)
    nf = HEAD_DIM // 4
    inv_freq = ROPE_THETA ** (-jnp.arange(nf, dtype=jnp.float32) / nf)
    ang = jnp.stack([row[:, None] * inv_freq, col[:, None] * inv_freq], axis=1)
    return jnp.cos(ang), jnp.sin(ang)


def apply_axial_rope(x, cos, sin):
    nf = HEAD_DIM // 4
    xs = x.astype(jnp.float32).reshape(*x.shape[:-1], 2, 2, nf)
    x1, x2 = xs[..., 0, :], xs[..., 1, :]
    c, s = cos[None, :, None], sin[None, :, None]
    out = jnp.stack([x1 * c - x2 * s, x2 * c + x1 * s], axis=-2)
    return out.reshape(x.shape).astype(x.dtype)


def t5_buckets(rel):
    nb = NUM_BUCKETS // 2
    max_exact = nb // 2
    ret = (rel > 0).astype(np.int32) * nb
    n = np.abs(rel)
    large = max_exact + (np.log(np.maximum(n, 1) / max_exact)
                         / np.log(MAX_DISTANCE / max_exact) * (nb - max_exact)).astype(np.int32)
    large = np.minimum(large, nb - 1)
    return (ret + np.where(n < max_exact, n, large)).astype(np.int32)


def gqa_axial_attention(q, k, v, cos, sin, g_q, g_k):
    bsz, seq_len = q.shape[:2]
    q = apply_axial_rope(rms_norm(q, g_q), cos, sin)
    k = apply_axial_rope(rms_norm(k, g_k), cos, sin)
    grp = A_Q_HEADS // A_KV_HEADS
    nblk = seq_len // QBLOCK
    qb = jnp.moveaxis(q.reshape(bsz, nblk, QBLOCK, A_KV_HEADS, grp, HEAD_DIM), 1, 0)
    scale = HEAD_DIM ** -0.5

    def block(qblk):
        s = jnp.einsum('bqkgd,bskd->bkgqs', qblk, k).astype(jnp.float32) * scale
        p = jax.nn.softmax(s, axis=-1).astype(v.dtype)
        return jnp.einsum('bkgqs,bskd->bqkgd', p, v)

    o = lax.map(block, qb)
    return jnp.moveaxis(o, 0, 1).reshape(bsz, seq_len, A_Q_W)


def dilated_group_attention(q, k, v, bias, dilation, side):
    bsz, seq_len, nh, dh = q.shape
    offsets = dilation * jnp.arange(-side, side + 1)
    nblk = seq_len // QBLOCK
    qb = jnp.moveaxis(q.reshape(bsz, nblk, QBLOCK, nh, dh), 1, 0)
    starts = jnp.arange(nblk) * QBLOCK
    scale = dh ** -0.5

    def block(args):
        qblk, start = args
        idx = start + jnp.arange(QBLOCK)[:, None] + offsets[None, :]
        valid = (idx >= 0) & (idx < seq_len)
        idx = jnp.clip(idx, 0, seq_len - 1)
        kg = k[:, idx]
        vg = v[:, idx]
        s = jnp.einsum('bqhd,bqwhd->bhqw', qblk, kg).astype(jnp.float32) * scale
        s = jnp.where(valid[None, None], s + bias[None, :, None, :], NEG)
        m = jnp.max(s, axis=-1, keepdims=True)
        p = jnp.exp(s - m)
        l = jnp.sum(p, axis=-1)
        o = jnp.einsum('bhqw,bqwhd->bqhd', p.astype(vg.dtype), vg).astype(jnp.float32)
        o = o / jnp.transpose(l, (0, 2, 1))[..., None]
        lse = jnp.transpose(m[..., 0] + jnp.log(l), (0, 2, 1))
        return o, lse

    o, lse = lax.map(block, (qb, starts))
    o = jnp.moveaxis(o, 0, 1).reshape(bsz, seq_len, nh, dh)
    lse = jnp.moveaxis(lse, 0, 1).reshape(bsz, seq_len, nh)
    return o, lse


def dilated_mixture(q, k, v, rel_bias):
    bsz, seq_len = q.shape[:2]
    outs, lses = [], []
    for g in range(N_DIL):
        side = DIL_WINDOWS[g] // (2 * DIL_RATES[g])
        rel = DIL_RATES[g] * np.arange(-side, side + 1)
        buckets = jnp.asarray(t5_buckets(rel))
        bias = rel_bias[:, g * B_HEADS:(g + 1) * B_HEADS].astype(jnp.float32)[buckets].T
        o, lse = dilated_group_attention(q[:, :, g], k[:, :, g], v[:, :, g], bias,
                                         DIL_RATES[g], side)
        outs.append(o)
        lses.append(lse)
    w = jax.nn.softmax(jnp.stack(lses, 0), axis=0)
    o = jnp.sum(w[..., None] * jnp.stack(outs, 0), axis=0)
    return o.reshape(bsz, seq_len, B_W).astype(q.dtype)


def memory_cross_attention(h, hm, w_q, w_kv, w_o):
    bsz, seq_len = h.shape[:2]
    q = (h @ w_q).reshape(bsz, seq_len, MEM_HEADS, HEAD_DIM)
    kv = hm @ w_kv
    k = kv[..., :MEM_W].reshape(bsz, -1, MEM_HEADS, HEAD_DIM)
    v = kv[..., MEM_W:].reshape(bsz, -1, MEM_HEADS, HEAD_DIM)
    s = jnp.einsum('bshd,bmhd->bhsm', q, k).astype(jnp.float32) * HEAD_DIM ** -0.5
    p = jax.nn.softmax(s, axis=-1).astype(v.dtype)
    o = jnp.einsum('bhsm,bmhd->bshd', p, v).reshape(bsz, seq_len, MEM_W)
    return o @ w_o


def setup_inputs(seed: int = 0) -> dict:
    key = jax.random.key(seed)
    ks = jax.random.split(key, 24)
    f32 = jnp.float32

    def dense(k, shape, fan_in):
        return jax.random.normal(k, shape, f32) * (fan_in ** -0.5)

    def gain(k, shape):
        return 1.0 + 0.05 * jax.random.normal(k, shape, f32)

    L = DEPTH
    return {
        "x": jax.random.normal(ks[0], (BATCH, SEQ, D_MODEL), f32),
        "mem": jax.random.normal(ks[1], (BATCH, MEM_TOKENS, D_MODEL), f32),
        "ffn1_norm": gain(ks[2], (L, D_MODEL)),
        "ffn1_w_gate": dense(ks[3], (L, D_MODEL, D_FF), D_MODEL),
        "ffn1_w_up": dense(ks[4], (L, D_MODEL, D_FF), D_MODEL),
        "ffn1_w_down": dense(ks[5], (L, D_FF, D_MODEL), D_FF),
        "mix_norm": gain(ks[6], (L, D_MODEL)),
        "w_in": dense(ks[7], (L, D_MODEL, IN_W), D_MODEL),
        "q_norm_a": gain(ks[8], (L, HEAD_DIM)),
        "k_norm_a": gain(ks[9], (L, HEAD_DIM)),
        "rel_bias": 0.1 * jax.random.normal(ks[10], (NUM_BUCKETS, N_DIL * B_HEADS), f32),
        "w_out": dense(ks[11], (L, MIX_W, D_MODEL), MIX_W),
        "mem_x_norm": gain(ks[12], (L, D_MODEL)),
        "mem_m_norm": gain(ks[13], (L, D_MODEL)),
        "w_q_mem": dense(ks[14], (L, D_MODEL, MEM_W), D_MODEL),
        "w_kv_mem": dense(ks[15], (L, D_MODEL, 2 * MEM_W), D_MODEL),
        "w_o_mem": dense(ks[16], (L, MEM_W, D_MODEL), MEM_W),
        "ffn2_norm": gain(ks[17], (L, D_MODEL)),
        "ffn2_w_gate": dense(ks[18], (L, D_MODEL, D_FF), D_MODEL),
        "ffn2_w_up": dense(ks[19], (L, D_MODEL, D_FF), D_MODEL),
        "ffn2_w_down": dense(ks[20], (L, D_FF, D_MODEL), D_FF),
        "final_norm": gain(ks[21], (D_MODEL,)),
    }


def reference(x, mem, ffn1_norm, ffn1_w_gate, ffn1_w_up, ffn1_w_down, mix_norm, w_in,
              q_norm_a, k_norm_a, rel_bias, w_out, mem_x_norm, mem_m_norm, w_q_mem,
              w_kv_mem, w_o_mem, ffn2_norm, ffn2_w_gate, ffn2_w_up, ffn2_w_down, final_norm):
    bsz, seq_len, _ = x.shape
    cos, sin = axial_rope_tables(seq_len)
    o_aq = A_Q_W
    o_ak = o_aq + A_KV_W
    o_av = o_ak + A_KV_W
    o_bq = o_av + N_DIL * B_W
    o_bk = o_bq + N_DIL * B_W
    for l in range(DEPTH):
        x = x + 0.5 * swiglu(rms_norm(x, ffn1_norm[l]), ffn1_w_gate[l], ffn1_w_up[l], ffn1_w_down[l])

        h = rms_norm(x, mix_norm[l])
        proj = h @ w_in[l]
        qa = proj[..., :o_aq].reshape(bsz, seq_len, A_Q_HEADS, HEAD_DIM)
        ka = proj[..., o_aq:o_ak].reshape(bsz, seq_len, A_KV_HEADS, HEAD_DIM)
        va = proj[..., o_ak:o_av].reshape(bsz, seq_len, A_KV_HEADS, HEAD_DIM)
        qb = proj[..., o_av:o_bq].reshape(bsz, seq_len, N_DIL, B_HEADS, HEAD_DIM)
        kb = proj[..., o_bq:o_bk].reshape(bsz, seq_len, N_DIL, B_HEADS, HEAD_DIM)
        vb = proj[..., o_bk:].reshape(bsz, seq_len, N_DIL, B_HEADS, HEAD_DIM)
        out_a = gqa_axial_attention(qa, ka, va, cos, sin, q_norm_a[l], k_norm_a[l])
        out_b = dilated_mixture(qb, kb, vb, rel_bias)
        x = x + jnp.concatenate([out_a, out_b], axis=-1) @ w_out[l]

        x = x + memory_cross_attention(rms_norm(x, mem_x_norm[l]), rms_norm(mem, mem_m_norm[l]),
                                       w_q_mem[l], w_kv_mem[l], w_o_mem[l])

        x = x + 0.5 * swiglu(rms_norm(x, ffn2_norm[l]), ffn2_w_gate[l], ffn2_w_up[l], ffn2_w_down[l])
    return rms_norm(x, final_norm)
```

```python
import functools
import math

import numpy as np
import jax
import jax.numpy as jnp
from jax import lax
from jax.experimental import pallas as pl
from jax.experimental.pallas import tpu as pltpu

F32 = jnp.float32
BF16 = jnp.bfloat16

HEAD_DIM = 128
A_Q_HEADS = 8
A_KV_HEADS = 2
A_GROUP = A_Q_HEADS // A_KV_HEADS
B_HEADS = 8
DIL_WINDOWS = (128, 512, 2048)
DIL_RATES = (1, 4, 16)
N_DIL = 3
MEM_HEADS = 4
NUM_BUCKETS = 32
MAX_DISTANCE = 1024
GRID_W = 64
ROPE_THETA = 10000.0
EPS = 1e-6
NEG = -1e30
SCALE = HEAD_DIM ** -0.5
LOG2E = math.log2(math.e)

A_Q_W = A_Q_HEADS * HEAD_DIM
A_KV_W = A_KV_HEADS * HEAD_DIM
B_W = B_HEADS * HEAD_DIM
A_W = A_Q_W + 2 * A_KV_W
MEM_W = MEM_HEADS * HEAD_DIM
DIL_SIDE = 64
DIL_QB = 128
DIL_KW = DIL_QB + 2 * DIL_SIDE
LSE_W = 128

VMEM_LIMIT = 56 * 1024 * 1024


def _cparams(*sem):
    return pltpu.CompilerParams(dimension_semantics=sem, vmem_limit_bytes=VMEM_LIMIT)


def _rms(x, g):
    return x * lax.rsqrt(jnp.mean(x * x, axis=-1, keepdims=True) + EPS) * g


def _dot(a, b):
    return jnp.dot(a, b, preferred_element_type=F32)


def _dot_nt(a, b):
    return lax.dot_general(a, b, (((1,), (1,)), ((), ())), preferred_element_type=F32)


def _ffn_kernel(x_ref, g_ref, wg_ref, wu_ref, wd_ref, g2_ref, *rest, emit_h):
    if emit_h:
        o_ref, h2_ref, h_scr, acc_scr = rest
    else:
        o_ref, h_scr, acc_scr = rest
    j = pl.program_id(1)

    @pl.when(j == 0)
    def _():
        h_scr[...] = _rms(x_ref[...], g_ref[...]).astype(BF16)

    h = h_scr[...]
    gate = _dot(h, wg_ref[...])
    up = _dot(h, wu_ref[...])
    act = (gate * jax.nn.sigmoid(gate) * up).astype(BF16)
    part = _dot(act, wd_ref[...])

    @pl.when(j == 0)
    def _():
        acc_scr[...] = part

    @pl.when(j > 0)
    def _():
        acc_scr[...] += part

    @pl.when(j == pl.num_programs(1) - 1)
    def _():
        y = x_ref[...] + 0.5 * acc_scr[...]
        if emit_h:
            o_ref[...] = y
            h2_ref[...] = _rms(y, g2_ref[...]).astype(BF16)
        else:
            o_ref[...] = _rms(y, g2_ref[...])


def _ffn(x, g, wg, wu, wd, g2, *, emit_h, tm, tf):
    t, d = x.shape
    f = wg.shape[1]
    out_shape = [jax.ShapeDtypeStruct((t, d), F32)]
    out_specs = [pl.BlockSpec((tm, d), lambda i, j: (i, 0))]
    if emit_h:
        out_shape.append(jax.ShapeDtypeStruct((t, d), BF16))
        out_specs.append(pl.BlockSpec((tm, d), lambda i, j: (i, 0)))
    res = pl.pallas_call(
        functools.partial(_ffn_kernel, emit_h=emit_h),
        grid=(t // tm, f // tf),
        in_specs=[
            pl.BlockSpec((tm, d), lambda i, j: (i, 0)),
            pl.BlockSpec((1, d), lambda i, j: (0, 0)),
            pl.BlockSpec((d, tf), lambda i, j: (0, j)),
            pl.BlockSpec((d, tf), lambda i, j: (0, j)),
            pl.BlockSpec((tf, d), lambda i, j: (j, 0)),
            pl.BlockSpec((1, d), lambda i, j: (0, 0)),
        ],
        out_specs=out_specs,
        out_shape=out_shape,
        scratch_shapes=[pltpu.VMEM((tm, d), BF16), pltpu.VMEM((tm, d), F32)],
        compiler_params=_cparams("parallel", "arbitrary"),
        name="ffn_h" if emit_h else "ffn_final",
    )(x, g, wg, wu, wd, g2)
    return res if emit_h else res[0]


def _proj_a_kernel(h_ref, w_ref, gq_ref, gk_ref, cos_ref, sin_ref, q_ref, k_ref, v_ref):
    acc = _dot(h_ref[...], w_ref[...])
    cos = cos_ref[...]
    sin = sin_ref[...]
    lane = lax.broadcasted_iota(jnp.int32, (1, HEAD_DIM), 1)
    first_half = (lane % (HEAD_DIM // 2)) < (HEAD_DIM // 4)

    def norm_rope(xh, g):
        xn = _rms(xh, g)
        swapped = jnp.where(first_half,
                            pltpu.roll(xn, HEAD_DIM - HEAD_DIM // 4, 1),
                            pltpu.roll(xn, HEAD_DIM // 4, 1))
        return xn * cos + swapped * sin

    gq = gq_ref[...]
    gk = gk_ref[...]
    for hd in range(A_Q_HEADS):
        cols = slice(hd * HEAD_DIM, (hd + 1) * HEAD_DIM)
        q_ref[:, cols] = (norm_rope(acc[:, cols], gq) * (SCALE * LOG2E)).astype(BF16)
    for hd in range(A_KV_HEADS):
        kc = slice(A_Q_W + hd * HEAD_DIM, A_Q_W + (hd + 1) * HEAD_DIM)
        vc = slice(A_Q_W + A_KV_W + hd * HEAD_DIM, A_Q_W + A_KV_W + (hd + 1) * HEAD_DIM)
        k_ref[hd] = norm_rope(acc[:, kc], gk).astype(BF16)
        v_ref[hd] = acc[:, vc].astype(BF16)


def _proj_a(h, w, gq, gk, cos, sin, *, tm):
    b, s, d = h.shape
    kv_shape = jax.ShapeDtypeStruct((b, A_KV_HEADS, s, HEAD_DIM), BF16)
    kv_spec = pl.BlockSpec((None, A_KV_HEADS, tm, HEAD_DIM), lambda bi, si: (bi, 0, si, 0))
    return pl.pallas_call(
        _proj_a_kernel,
        grid=(b, s // tm),
        in_specs=[
            pl.BlockSpec((None, tm, d), lambda bi, si: (bi, si, 0)),
            pl.BlockSpec((d, A_W), lambda bi, si: (0, 0)),
            pl.BlockSpec((1, HEAD_DIM), lambda bi, si: (0, 0)),
            pl.BlockSpec((1, HEAD_DIM), lambda bi, si: (0, 0)),
            pl.BlockSpec((tm, HEAD_DIM), lambda bi, si: (si, 0)),
            pl.BlockSpec((tm, HEAD_DIM), lambda bi, si: (si, 0)),
        ],
        out_specs=[pl.BlockSpec((None, tm, A_Q_W), lambda bi, si: (bi, si, 0)), kv_spec, kv_spec],
        out_shape=[jax.ShapeDtypeStruct((b, s, A_Q_W), BF16), kv_shape, kv_shape],
        compiler_params=_cparams("parallel", "parallel"),
        name="proj_a",
    )(h, w, gq, gk, cos, sin)


def _matmul_kernel(a_ref, b_ref, o_ref):
    o_ref[...] = _dot(a_ref[...], b_ref[...]).astype(o_ref.dtype)


def _proj_b(h, w, *, tm, tn):
    t, d = h.shape
    n = w.shape[1]
    return pl.pallas_call(
        _matmul_kernel,
        grid=(t // tm, n // tn),
        in_specs=[pl.BlockSpec((tm, d), lambda i, j: (i, 0)),
                  pl.BlockSpec((d, tn), lambda i, j: (0, j))],
        out_specs=pl.BlockSpec((tm, tn), lambda i, j: (i, j)),
        out_shape=jax.ShapeDtypeStruct((t, n), BF16),
        compiler_params=_cparams("parallel", "parallel"),
        name="proj_b",
    )(h, w)


def _attn_a_kernel(q_ref, k_ref, v_ref, o_ref, m_scr, l_scr, acc_scr, *, tk, nk):
    for hd in range(A_GROUP):
        cols = slice(hd * HEAD_DIM, (hd + 1) * HEAD_DIM)
        q = q_ref[:, cols]
        m_scr[...] = jnp.full(m_scr.shape, -jnp.inf, F32)
        l_scr[...] = jnp.zeros(l_scr.shape, F32)
        acc_scr[...] = jnp.zeros(acc_scr.shape, F32)

        def body(kb, carry):
            off = pl.multiple_of(kb * tk, tk)
            k = k_ref[pl.ds(off, tk), :]
            v = v_ref[pl.ds(off, tk), :]
            s = _dot_nt(q, k)
            m_prev = m_scr[...]
            m_new = jnp.maximum(m_prev, jnp.max(s, axis=-1, keepdims=True))
            alpha = jnp.exp2(m_prev - m_new)
            p = jnp.exp2(s - m_new)
            l_scr[...] = alpha * l_scr[...] + jnp.sum(p, axis=-1, keepdims=True)
            acc_scr[...] = alpha * acc_scr[...] + _dot(p.astype(BF16), v)
            m_scr[...] = m_new
            return carry

        lax.fori_loop(0, nk, body, 0)
        o_ref[:, cols] = (acc_scr[...] / l_scr[...]).astype(o_ref.dtype)


def _attn_a(q, k, v, *, tq, tk):
    b, s, _ = q.shape
    gw = A_GROUP * HEAD_DIM
    kv_spec = pl.BlockSpec((None, None, s, HEAD_DIM), lambda bi, hi, qi: (bi, hi, 0, 0))
    return pl.pallas_call(
        functools.partial(_attn_a_kernel, tk=tk, nk=s // tk),
        grid=(b, A_KV_HEADS, s // tq),
        in_specs=[pl.BlockSpec((None, tq, gw), lambda bi, hi, qi: (bi, qi, hi)), kv_spec, kv_spec],
        out_specs=pl.BlockSpec((None, tq, gw), lambda bi, hi, qi: (bi, qi, hi)),
        out_shape=jax.ShapeDtypeStruct((b, s, A_Q_W), BF16),
        scratch_shapes=[pltpu.VMEM((tq, 1), F32), pltpu.VMEM((tq, 1), F32),
                        pltpu.VMEM((tq, HEAD_DIM), F32)],
        compiler_params=_cparams("parallel", "parallel", "arbitrary"),
        name="attn_a",
    )(q, k, v)


def _t5_buckets(rel):
    nb = NUM_BUCKETS // 2
    max_exact = nb // 2
    ret = (rel > 0).astype(np.int32) * nb
    n = np.abs(rel)
    large = max_exact + (np.log(np.maximum(n, 1) / max_exact)
                         / np.log(MAX_DISTANCE / max_exact) * (nb - max_exact)).astype(np.int32)
    large = np.minimum(large, nb - 1)
    return (ret + np.where(n < max_exact, n, large)).astype(np.int32)


def _band_bias(rel_bias, g):
    rel = DIL_RATES[g] * np.arange(-DIL_SIDE, DIL_SIDE + 1)
    buckets = _t5_buckets(rel)
    w = np.arange(DIL_KW)[None, :] - np.arange(DIL_QB)[:, None]
    on_band = (w >= 0) & (w <= 2 * DIL_SIDE)
    idx = buckets[np.clip(w, 0, 2 * DIL_SIDE)]
    bias = rel_bias[:, g * B_HEADS:(g + 1) * B_HEADS].astype(F32)[idx]
    bias = jnp.where(on_band[:, :, None], bias, NEG)
    return jnp.transpose(bias, (2, 0, 1))


def _dil_kernel(*refs, tu, length, first, last):
    q_ref, kp_ref, kc_ref, kn_ref, vp_ref, vc_ref, vn_ref, bias_ref = refs[:8]
    refs = refs[8:]
    if not first:
        op_ref, lp_ref = refs[:2]
        refs = refs[2:]
    if last:
        o_ref, kbuf, vbuf = refs
    else:
        o_ref, lse_ref, kbuf, vbuf = refs

    kbuf[0:DIL_SIDE] = kp_ref[...]
    kbuf[DIL_SIDE:DIL_SIDE + tu] = kc_ref[...]
    kbuf[DIL_SIDE + tu:] = kn_ref[...]
    vbuf[0:DIL_SIDE] = vp_ref[...]
    vbuf[DIL_SIDE:DIL_SIDE + tu] = vc_ref[...]
    vbuf[DIL_SIDE + tu:] = vn_ref[...]

    u0 = pl.program_id(2) * tu
    kk = lax.broadcasted_iota(jnp.int32, (DIL_QB, DIL_KW), 1)
    lane_h = lax.broadcasted_iota(jnp.int32, (DIL_QB, LSE_W), 1)
    for qi in range(tu // DIL_QB):
        rows = slice(qi * DIL_QB, (qi + 1) * DIL_QB)
        key_u = u0 + (qi * DIL_QB - DIL_SIDE) + kk
        in_seq = (key_u >= 0) & (key_u < length)
        if not first:
            lse_prev = lp_ref[rows, :]
        lse_out = jnp.zeros((DIL_QB, LSE_W), F32)
        for hd in range(B_HEADS):
            cols = slice(hd * HEAD_DIM, (hd + 1) * HEAD_DIM)
            q = q_ref[rows, cols]
            kw = kbuf[qi * DIL_QB:qi * DIL_QB + DIL_KW, cols]
            vw = vbuf[qi * DIL_QB:qi * DIL_QB + DIL_KW, cols]
            s = _dot_nt(q, kw) * SCALE
            s = jnp.where(in_seq, s + bias_ref[hd], NEG)
            m = jnp.max(s, axis=-1, keepdims=True)
            p = jnp.exp(s - m)
            l = jnp.sum(p, axis=-1, keepdims=True)
            o = _dot(p.astype(BF16), vw) / l
            lse = m + jnp.log(l)
            if not first:
                lp = lse_prev[:, hd:hd + 1]
                mx = jnp.maximum(lp, lse)
                ea = jnp.exp(lp - mx)
                eb = jnp.exp(lse - mx)
                den = ea + eb
                o = (op_ref[rows, cols] * ea + o * eb) / den
                lse = mx + jnp.log(den)
            o_ref[rows, cols] = o.astype(o_ref.dtype)
            if not last:
                lse_out = jnp.where(lane_h == hd, lse, lse_out)
        if not last:
            lse_ref[rows, :] = lse_out


def _dil_group(pb, bias, prev, g, *, last):
    b, s, pw = pb.shape
    r = DIL_RATES[g]
    length = s // r
    tu = min(512, length)
    hb = tu // DIL_SIDE
    nhb = length // DIL_SIDE
    first = prev is None
    pbv = pb.reshape(b, length, r * pw)
    ncol = pw // B_W
    qcol, kcol, vcol = g, N_DIL + g, 2 * N_DIL + g

    def cur(col):
        return pl.BlockSpec((None, tu, B_W), lambda bi, ci, ui: (bi, ui, ci * ncol + col))

    def halo_prev(col):
        return pl.BlockSpec((None, DIL_SIDE, B_W),
                            lambda bi, ci, ui: (bi, jnp.maximum(ui * hb - 1, 0), ci * ncol + col))

    def halo_next(col):
        return pl.BlockSpec((None, DIL_SIDE, B_W),
                            lambda bi, ci, ui: (bi, jnp.minimum((ui + 1) * hb, nhb - 1),
                                                ci * ncol + col))

    o_spec = pl.BlockSpec((None, tu, B_W), lambda bi, ci, ui: (bi, ui, ci))
    lse_spec = pl.BlockSpec((None, tu, LSE_W), lambda bi, ci, ui: (bi, ui, ci))
    in_specs = [cur(qcol), halo_prev(kcol), cur(kcol), halo_next(kcol),
                halo_prev(vcol), cur(vcol), halo_next(vcol),
                pl.BlockSpec((B_HEADS, DIL_QB, DIL_KW), lambda bi, ci, ui: (0, 0, 0))]
    args = [pbv] * 7 + [bias]
    if not first:
        in_specs += [o_spec, lse_spec]
        args += [prev[0].reshape(b, length, r * B_W), prev[1].reshape(b, length, r * LSE_W)]
    if last:
        out_specs = o_spec
        out_shape = jax.ShapeDtypeStruct((b, length, r * B_W), BF16)
    else:
        out_specs = [o_spec, lse_spec]
        out_shape = [jax.ShapeDtypeStruct((b, length, r * B_W), F32),
                     jax.ShapeDtypeStruct((b, length, r * LSE_W), F32)]
    res = pl.pallas_call(
        functools.partial(_dil_kernel, tu=tu, length=length, first=first, last=last),
        grid=(b, r, length // tu),
        in_specs=in_specs,
        out_specs=out_specs,
        out_shape=out_shape,
        scratch_shapes=[pltpu.VMEM((tu + 2 * DIL_SIDE, B_W), BF16),
                        pltpu.VMEM((tu + 2 * DIL_SIDE, B_W), BF16)],
        compiler_params=_cparams("parallel", "parallel", "parallel"),
        name=f"dil_g{g}",
    )(*args)
    if last:
        return res.reshape(b, s, B_W)
    return res[0].reshape(b, s, B_W), res[1].reshape(b, s, LSE_W)


def _memkv_kernel(m_ref, g_ref, w_ref, k_ref, v_ref):
    hm = _rms(m_ref[...], g_ref[...]).astype(BF16)
    kv = _dot(hm, w_ref[...])
    k_ref[...] = kv[:, :MEM_W].astype(BF16)
    v_ref[...] = kv[:, MEM_W:].astype(BF16)


def _memkv(mem, g, w):
    b, m, d = mem.shape
    spec = pl.BlockSpec((None, m, MEM_W), lambda bi: (bi, 0, 0))
    shape = jax.ShapeDtypeStruct((b, m, MEM_W), BF16)
    return pl.pallas_call(
        _memkv_kernel,
        grid=(b,),
        in_specs=[pl.BlockSpec((None, m, d), lambda bi: (bi, 0, 0)),
                  pl.BlockSpec((1, d), lambda bi: (0, 0)),
                  pl.BlockSpec((d, 2 * MEM_W), lambda bi: (0, 0))],
        out_specs=[spec, spec],
        out_shape=[shape, shape],
        compiler_params=_cparams("parallel"),
        name="memkv",
    )(mem, g, w)


def _mid_kernel(x_ref, a_ref, b_ref, wout_ref, g_ref, wq_ref, km_ref, vm_ref, wo_ref, o_ref):
    x2 = (x_ref[...] + _dot(a_ref[...], wout_ref[0:A_Q_W, :])
          + _dot(b_ref[...], wout_ref[A_Q_W:, :]))
    h = _rms(x2, g_ref[...]).astype(BF16)
    q = _dot(h, wq_ref[...]).astype(BF16)
    outs = []
    for hd in range(MEM_HEADS):
        cols = slice(hd * HEAD_DIM, (hd + 1) * HEAD_DIM)
        s = _dot_nt(q[:, cols], km_ref[:, cols]) * SCALE
        m = jnp.max(s, axis=-1, keepdims=True)
        p = jnp.exp(s - m)
        p = p / jnp.sum(p, axis=-1, keepdims=True)
        outs.append(_dot(p.astype(BF16), vm_ref[:, cols]).astype(BF16))
    o = jnp.concatenate(outs, axis=-1)
    o_ref[...] = x2 + _dot(o, wo_ref[...])


def _mid(x, a, bmix, wout, g, wq, km, vm, wo, *, tm):
    b, s, d = x.shape
    m = km.shape[1]
    const = lambda bi, si: (0, 0)
    return pl.pallas_call(
        _mid_kernel,
        grid=(b, s // tm),
        in_specs=[
            pl.BlockSpec((None, tm, d), lambda bi, si: (bi, si, 0)),
            pl.BlockSpec((None, tm, A_Q_W), lambda bi, si: (bi, si, 0)),
            pl.BlockSpec((None, tm, B_W), lambda bi, si: (bi, si, 0)),
            pl.BlockSpec((A_Q_W + B_W, d), const),
            pl.BlockSpec((1, d), const),
            pl.BlockSpec((d, MEM_W), const),
            pl.BlockSpec((None, m, MEM_W), lambda bi, si: (bi, 0, 0)),
            pl.BlockSpec((None, m, MEM_W), lambda bi, si: (bi, 0, 0)),
            pl.BlockSpec((MEM_W, d), const),
        ],
        out_specs=pl.BlockSpec((None, tm, d), lambda bi, si: (bi, si, 0)),
        out_shape=jax.ShapeDtypeStruct((b, s, d), F32),
        compiler_params=_cparams("parallel", "parallel"),
        name="mid",
    )(x, a, bmix, wout, g, wq, km, vm, wo)


def _rope_tables(seq_len):
    nf = HEAD_DIM // 4
    t = jnp.arange(seq_len)
    row = (t // GRID_W).astype(F32)
    col = (t % GRID_W).astype(F32)
    inv_freq = ROPE_THETA ** (-jnp.arange(nf, dtype=F32) / nf)
    ar = row[:, None] * inv_freq
    ac = col[:, None] * inv_freq
    cos = jnp.concatenate([jnp.cos(ar), jnp.cos(ar), jnp.cos(ac), jnp.cos(ac)], axis=-1)
    sin = jnp.concatenate([-jnp.sin(ar), jnp.sin(ar), -jnp.sin(ac), jnp.sin(ac)], axis=-1)
    return cos, sin


def _tile(n, pref):
    return pref if n % pref == 0 else n


def kernel(x, mem, ffn1_norm, ffn1_w_gate, ffn1_w_up, ffn1_w_down, mix_norm, w_in, q_norm_a, k_norm_a, rel_bias, w_out, mem_x_norm, mem_m_norm, w_q_mem, w_kv_mem, w_o_mem, ffn2_norm, ffn2_w_gate, ffn2_w_up, ffn2_w_down, final_norm):
    bsz, seq_len, d = x.shape
    depth = ffn1_norm.shape[0]
    t = bsz * seq_len
    tm = _tile(t, 512)
    ts = _tile(seq_len, 512)
    tf = _tile(ffn1_w_gate.shape[-1], 512)
    cos, sin = _rope_tables(seq_len)
    row = lambda v: v.reshape(1, -1).astype(F32)
    bf = lambda w: w.astype(BF16)

    xf = x.reshape(t, d)
    for l in range(depth):
        x1, hmix = _ffn(xf, row(ffn1_norm[l]), bf(ffn1_w_gate[l]), bf(ffn1_w_up[l]),
                        bf(ffn1_w_down[l]), row(mix_norm[l]), emit_h=True, tm=tm, tf=tf)
        w_in_l = bf(w_in[l])
        qa, ka, va = _proj_a(hmix.reshape(bsz, seq_len, d), w_in_l[:, :A_W], row(q_norm_a[l]),
                             row(k_norm_a[l]), cos, sin, tm=ts)
        pb = _proj_b(hmix, w_in_l[:, A_W:], tm=_tile(t, 1024), tn=1024)
        out_a = _attn_a(qa, ka, va, tq=ts, tk=ts)
        state = None
        pb3 = pb.reshape(bsz, seq_len, -1)
        for g in range(N_DIL):
            state = _dil_group(pb3, _band_bias(rel_bias, g), state, g, last=(g == N_DIL - 1))
        out_b = state
        km, vm = _memkv(mem, row(mem_m_norm[l]), bf(w_kv_mem[l]))
        x3 = _mid(x1.reshape(bsz, seq_len, d), out_a, out_b, bf(w_out[l]), row(mem_x_norm[l]),
                  bf(w_q_mem[l]), km, vm, bf(w_o_mem[l]), tm=ts)
        x3 = x3.reshape(t, d)
        if l == depth - 1:
            xf = _ffn(x3, row(ffn2_norm[l]), bf(ffn2_w_gate[l]), bf(ffn2_w_up[l]),
                      bf(ffn2_w_down[l]), row(final_norm), emit_h=False, tm=tm, tf=tf)
        else:
            xf, _ = _ffn(x3, row(ffn2_norm[l]), bf(ffn2_w_gate[l]), bf(ffn2_w_up[l]),
                         bf(ffn2_w_down[l]), row(ffn1_norm[l + 1]), emit_h=True, tm=tm, tf=tf)
    return xf.reshape(bsz, seq_len, d)
```

```python
import functools
import math

import numpy as np
import jax
import jax.numpy as jnp
from jax import lax
from jax.experimental import pallas as pl
from jax.experimental.pallas import tpu as pltpu

F32 = jnp.float32
BF16 = jnp.bfloat16

HEAD_DIM = 128
A_Q_HEADS = 8
A_KV_HEADS = 2
A_GROUP = A_Q_HEADS // A_KV_HEADS
B_HEADS = 8
DIL_WINDOWS = (128, 512, 2048)
DIL_RATES = (1, 4, 16)
N_DIL = 3
MEM_HEADS = 4
NUM_BUCKETS = 32
MAX_DISTANCE = 1024
GRID_W = 64
ROPE_THETA = 10000.0
EPS = 1e-6
NEG = -1e30
SCALE = HEAD_DIM ** -0.5
LOG2E = math.log2(math.e)

A_Q_W = A_Q_HEADS * HEAD_DIM
A_KV_W = A_KV_HEADS * HEAD_DIM
B_W = B_HEADS * HEAD_DIM
A_W = A_Q_W + 2 * A_KV_W
MEM_W = MEM_HEADS * HEAD_DIM
DIL_SIDE = 64
DIL_QB = 128
DIL_KW = DIL_QB + 2 * DIL_SIDE
LSE_W = 128

LANES = 128
VMEM_LIMIT = 56 * 1024 * 1024


def _cparams(*sem):
    return pltpu.CompilerParams(dimension_semantics=sem, vmem_limit_bytes=VMEM_LIMIT)


def _rms(x, g):
    return x * lax.rsqrt(jnp.mean(x * x, axis=-1, keepdims=True) + EPS) * g


def _dot(a, b):
    return jnp.dot(a, b, preferred_element_type=F32)


def _dot_nt(a, b):
    return lax.dot_general(a, b, (((1,), (1,)), ((), ())), preferred_element_type=F32)


def _ffn_kernel(x_ref, g_ref, wg_ref, wu_ref, wd_ref, g2_ref, *rest, emit_h):
    if emit_h:
        o_ref, h2_ref, h_scr, acc_scr = rest
    else:
        o_ref, h_scr, acc_scr = rest
    j = pl.program_id(1)

    @pl.when(j == 0)
    def _():
        h_scr[...] = _rms(x_ref[...], g_ref[...]).astype(BF16)

    h = h_scr[...]
    gate = _dot(h, wg_ref[...])
    up = _dot(h, wu_ref[...])
    act = (gate * jax.nn.sigmoid(gate) * up).astype(BF16)
    part = _dot(act, wd_ref[...])

    @pl.when(j == 0)
    def _():
        acc_scr[...] = part

    @pl.when(j > 0)
    def _():
        acc_scr[...] += part

    @pl.when(j == pl.num_programs(1) - 1)
    def _():
        y = x_ref[...] + 0.5 * acc_scr[...]
        if emit_h:
            o_ref[...] = y
            h2_ref[...] = _rms(y, g2_ref[...]).astype(BF16)
        else:
            o_ref[...] = _rms(y, g2_ref[...])


def _ffn(x, g, wg, wu, wd, g2, *, emit_h, tm, tf):
    t, d = x.shape
    f = wg.shape[1]
    out_shape = [jax.ShapeDtypeStruct((t, d), F32)]
    out_specs = [pl.BlockSpec((tm, d), lambda i, j: (i, 0))]
    if emit_h:
        out_shape.append(jax.ShapeDtypeStruct((t, d), BF16))
        out_specs.append(pl.BlockSpec((tm, d), lambda i, j: (i, 0)))
    res = pl.pallas_call(
        functools.partial(_ffn_kernel, emit_h=emit_h),
        grid=(t // tm, f // tf),
        in_specs=[
            pl.BlockSpec((tm, d), lambda i, j: (i, 0)),
            pl.BlockSpec((1, d), lambda i, j: (0, 0)),
            pl.BlockSpec((d, tf), lambda i, j: (0, j)),
            pl.BlockSpec((d, tf), lambda i, j: (0, j)),
            pl.BlockSpec((tf, d), lambda i, j: (j, 0)),
            pl.BlockSpec((1, d), lambda i, j: (0, 0)),
        ],
        out_specs=out_specs,
        out_shape=out_shape,
        scratch_shapes=[pltpu.VMEM((tm, d), BF16), pltpu.VMEM((tm, d), F32)],
        compiler_params=_cparams("parallel", "arbitrary"),
        name="ffn_h" if emit_h else "ffn_final",
    )(x, g, wg, wu, wd, g2)
    return res if emit_h else res[0]


def _proj_a_kernel(h_ref, w_ref, gq_ref, gk_ref, cos_ref, sin_ref, q_ref, k_ref, v_ref):
    acc = _dot(h_ref[...], w_ref[...])
    cos = cos_ref[...]
    sin = sin_ref[...]
    lane = lax.broadcasted_iota(jnp.int32, (1, HEAD_DIM), 1)
    first_half = (lane % (HEAD_DIM // 2)) < (HEAD_DIM // 4)

    def norm_rope(xh, g):
        xn = _rms(xh, g)
        swapped = jnp.where(first_half,
                            pltpu.roll(xn, HEAD_DIM - HEAD_DIM // 4, 1),
                            pltpu.roll(xn, HEAD_DIM // 4, 1))
        return xn * cos + swapped * sin

    gq = gq_ref[...]
    gk = gk_ref[...]
    for hd in range(A_Q_HEADS):
        cols = slice(hd * HEAD_DIM, (hd + 1) * HEAD_DIM)
        q_ref[hd] = (norm_rope(acc[:, cols], gq) * (SCALE * LOG2E)).T.astype(BF16)
    for hd in range(A_KV_HEADS):
        kc = slice(A_Q_W + hd * HEAD_DIM, A_Q_W + (hd + 1) * HEAD_DIM)
        vc = slice(A_Q_W + A_KV_W + hd * HEAD_DIM, A_Q_W + A_KV_W + (hd + 1) * HEAD_DIM)
        k_ref[hd] = norm_rope(acc[:, kc], gk).astype(BF16)
        v_ref[hd] = acc[:, vc].T.astype(BF16)


def _proj_a(h, w, gq, gk, cos, sin, *, tm):
    b, s, d = h.shape
    k_shape = jax.ShapeDtypeStruct((b, A_KV_HEADS, s, HEAD_DIM), BF16)
    k_spec = pl.BlockSpec((None, A_KV_HEADS, tm, HEAD_DIM), lambda bi, si: (bi, 0, si, 0))
    vt_shape = jax.ShapeDtypeStruct((b, A_KV_HEADS, HEAD_DIM, s), BF16)
    vt_spec = pl.BlockSpec((None, A_KV_HEADS, HEAD_DIM, tm), lambda bi, si: (bi, 0, 0, si))
    qt_shape = jax.ShapeDtypeStruct((b, A_Q_HEADS, HEAD_DIM, s), BF16)
    qt_spec = pl.BlockSpec((None, A_Q_HEADS, HEAD_DIM, tm), lambda bi, si: (bi, 0, 0, si))
    return pl.pallas_call(
        _proj_a_kernel,
        grid=(b, s // tm),
        in_specs=[
            pl.BlockSpec((None, tm, d), lambda bi, si: (bi, si, 0)),
            pl.BlockSpec((d, A_W), lambda bi, si: (0, 0)),
            pl.BlockSpec((1, HEAD_DIM), lambda bi, si: (0, 0)),
            pl.BlockSpec((1, HEAD_DIM), lambda bi, si: (0, 0)),
            pl.BlockSpec((tm, HEAD_DIM), lambda bi, si: (si, 0)),
            pl.BlockSpec((tm, HEAD_DIM), lambda bi, si: (si, 0)),
        ],
        out_specs=[qt_spec, k_spec, vt_spec],
        out_shape=[qt_shape, k_shape, vt_shape],
        compiler_params=_cparams("parallel", "parallel"),
        name="proj_a",
    )(h, w, gq, gk, cos, sin)


def _proj_b_kernel(h_ref, w_ref, q_ref, k_ref, v_ref, acc_scr, *, r):
    acc = _dot(h_ref[...], w_ref[...])
    j = pl.program_id(1)
    if r == 1:
        for idx, o_ref in enumerate((q_ref, k_ref, v_ref)):
            @pl.when(j == idx)
            def _(o_ref=o_ref):
                o_ref[...] = acc.astype(BF16)
        return
    nchunk = acc_scr.shape[0]
    lanes = acc_scr.shape[2]
    for ch in range(nchunk):
        acc_scr[ch] = acc[:, ch * lanes:(ch + 1) * lanes]
    rows = acc_scr.shape[1] // r
    for idx, o_ref in enumerate((q_ref, k_ref, v_ref)):
        @pl.when(j == idx)
        def _(o_ref=o_ref):
            for c in range(r):
                for ch in range(nchunk):
                    col = c * B_W + ch * lanes
                    o_ref[:, col:col + lanes] = acc_scr[ch, pl.ds(c, rows, stride=r), :].astype(BF16)


def _proj_b(h, w, r, *, tm):
    t, d = h.shape
    spec = pl.BlockSpec((tm // r, r * B_W), lambda i, j: (i, 0))
    shape = jax.ShapeDtypeStruct((t // r, r * B_W), BF16)
    return pl.pallas_call(
        functools.partial(_proj_b_kernel, r=r),
        grid=(t // tm, 3),
        in_specs=[pl.BlockSpec((tm, d), lambda i, j: (i, 0)),
                  pl.BlockSpec((d, B_W), lambda i, j: (0, j))],
        out_specs=[spec, spec, spec],
        out_shape=[shape, shape, shape],
        scratch_shapes=[pltpu.VMEM((B_W // LANES, tm, LANES), F32)],
        compiler_params=_cparams("parallel", "arbitrary"),
        name=f"proj_b_r{r}",
    )(h, w)


def _attn_a_kernel(qt_ref, k_ref, vt_ref, o_ref, m_scr, l_scr, acc_scr, *, tk, nk):
    m_scr[...] = jnp.full(m_scr.shape, -jnp.inf, F32)
    l_scr[...] = jnp.zeros(l_scr.shape, F32)
    acc_scr[...] = jnp.zeros(acc_scr.shape, F32)

    def body(kb, carry):
        off = pl.multiple_of(kb * tk, tk)
        k = k_ref[pl.ds(off, tk), :]
        vt = vt_ref[:, pl.ds(off, tk)]

        def scores(hd):
            return _dot(k, qt_ref[hd])

        def softmax(hd, st):
            m_prev = m_scr[hd]
            m_new = jnp.maximum(m_prev, jnp.max(st, axis=0, keepdims=True))
            alpha = jnp.exp2(m_prev - m_new)
            p = jnp.exp2(st - m_new)
            l_scr[hd] = alpha * l_scr[hd] + jnp.sum(p, axis=0, keepdims=True)
            m_scr[hd] = m_new
            return p.astype(BF16), alpha

        def accumulate(hd, p, alpha):
            acc_scr[hd] = alpha * acc_scr[hd] + _dot(vt, p)

        st, pa = {}, {}
        for i in range(A_GROUP + 2):
            if i < A_GROUP:
                st[i] = scores(i)
            if 0 <= i - 1 < A_GROUP:
                pa[i - 1] = softmax(i - 1, st.pop(i - 1))
            if 0 <= i - 2 < A_GROUP:
                accumulate(i - 2, *pa.pop(i - 2))
        return carry

    lax.fori_loop(0, nk, body, 0)
    for hd in range(A_GROUP):
        o = acc_scr[hd] / l_scr[hd]
        o_ref[:, hd * HEAD_DIM:(hd + 1) * HEAD_DIM] = o.T.astype(o_ref.dtype)


def _attn_a(qt, k, vt, *, tq, tk):
    b, _, _, s = qt.shape
    gw = A_GROUP * HEAD_DIM
    return pl.pallas_call(
        functools.partial(_attn_a_kernel, tk=tk, nk=s // tk),
        grid=(b, A_KV_HEADS, s // tq),
        in_specs=[
            pl.BlockSpec((None, A_GROUP, HEAD_DIM, tq), lambda bi, hi, qi: (bi, hi, 0, qi)),
            pl.BlockSpec((None, None, s, HEAD_DIM), lambda bi, hi, qi: (bi, hi, 0, 0)),
            pl.BlockSpec((None, None, HEAD_DIM, s), lambda bi, hi, qi: (bi, hi, 0, 0)),
        ],
        out_specs=pl.BlockSpec((None, tq, gw), lambda bi, hi, qi: (bi, qi, hi)),
        out_shape=jax.ShapeDtypeStruct((b, s, A_Q_W), BF16),
        scratch_shapes=[pltpu.VMEM((A_GROUP, 1, tq), F32), pltpu.VMEM((A_GROUP, 1, tq), F32),
                        pltpu.VMEM((A_GROUP, HEAD_DIM, tq), F32)],
        compiler_params=_cparams("parallel", "parallel", "arbitrary"),
        name="attn_a",
    )(qt, k, vt)


def _t5_buckets(rel):
    nb = NUM_BUCKETS // 2
    max_exact = nb // 2
    ret = (rel > 0).astype(np.int32) * nb
    n = np.abs(rel)
    large = max_exact + (np.log(np.maximum(n, 1) / max_exact)
                         / np.log(MAX_DISTANCE / max_exact) * (nb - max_exact)).astype(np.int32)
    large = np.minimum(large, nb - 1)
    return (ret + np.where(n < max_exact, n, large)).astype(np.int32)


def _band_bias(rel_bias, g):
    rel = DIL_RATES[g] * np.arange(-DIL_SIDE, DIL_SIDE + 1)
    buckets = _t5_buckets(rel)
    w = np.arange(DIL_KW)[None, :] - np.arange(DIL_QB)[:, None]
    on_band = (w >= 0) & (w <= 2 * DIL_SIDE)
    idx = buckets[np.clip(w, 0, 2 * DIL_SIDE)]
    bias = rel_bias[:, g * B_HEADS:(g + 1) * B_HEADS].astype(F32)[idx]
    bias = jnp.where(on_band[:, :, None], bias, NEG)
    return jnp.transpose(bias, (2, 0, 1))


def _dil_kernel(*refs, tu, length, first, last):
    q_ref, kp_ref, kc_ref, kn_ref, vp_ref, vc_ref, vn_ref, bias_ref = refs[:8]
    refs = refs[8:]
    if not first:
        op_ref, lp_ref = refs[:2]
        refs = refs[2:]
    if last:
        o_ref, kbuf, vbuf = refs
    else:
        o_ref, lse_ref, kbuf, vbuf = refs

    kbuf[0:DIL_SIDE] = kp_ref[...]
    kbuf[DIL_SIDE:DIL_SIDE + tu] = kc_ref[...]
    kbuf[DIL_SIDE + tu:] = kn_ref[...]
    vbuf[0:DIL_SIDE] = vp_ref[...]
    vbuf[DIL_SIDE:DIL_SIDE + tu] = vc_ref[...]
    vbuf[DIL_SIDE + tu:] = vn_ref[...]

    u0 = pl.program_id(2) * tu
    kk = lax.broadcasted_iota(jnp.int32, (DIL_QB, DIL_KW), 1)
    lane_h = lax.broadcasted_iota(jnp.int32, (DIL_QB, LSE_W), 1)
    for qi in range(tu // DIL_QB):
        rows = slice(qi * DIL_QB, (qi + 1) * DIL_QB)
        key_u = u0 + (qi * DIL_QB - DIL_SIDE) + kk
        in_seq = (key_u >= 0) & (key_u < length)
        if not first:
            lse_prev = lp_ref[rows, :]
        lse_out = jnp.zeros((DIL_QB, LSE_W), F32)
        for hd in range(B_HEADS):
            cols = slice(hd * HEAD_DIM, (hd + 1) * HEAD_DIM)
            q = q_ref[rows, cols]
            kw = kbuf[qi * DIL_QB:qi * DIL_QB + DIL_KW, cols]
            vw = vbuf[qi * DIL_QB:qi * DIL_QB + DIL_KW, cols]
            s = _dot_nt(q, kw) * SCALE
            s = jnp.where(in_seq, s + bias_ref[hd], NEG)
            m = jnp.max(s, axis=-1, keepdims=True)
            p = jnp.exp(s - m)
            l = jnp.sum(p, axis=-1, keepdims=True)
            o = _dot(p.astype(BF16), vw) / l
            lse = m + jnp.log(l)
            if not first:
                lp = lse_prev[:, hd:hd + 1]
                mx = jnp.maximum(lp, lse)
                ea = jnp.exp(lp - mx)
                eb = jnp.exp(lse - mx)
                den = ea + eb
                o = (op_ref[rows, cols] * ea + o * eb) / den
                lse = mx + jnp.log(den)
            o_ref[rows, cols] = o.astype(o_ref.dtype)
            if not last:
                lse_out = jnp.where(lane_h == hd, lse, lse_out)
        if not last:
            lse_ref[rows, :] = lse_out


def _dil_group(qkv, bias, prev, g, b, s, *, last):
    r = DIL_RATES[g]
    length = s // r
    tu = min(512, length)
    hb = tu // DIL_SIDE
    nhb = length // DIL_SIDE
    first = prev is None
    qv, kv, vv = (a.reshape(b, length, r * B_W) for a in qkv)

    cur = pl.BlockSpec((None, tu, B_W), lambda bi, ci, ui: (bi, ui, ci))
    halo_prev = pl.BlockSpec((None, DIL_SIDE, B_W),
                             lambda bi, ci, ui: (bi, jnp.maximum(ui * hb - 1, 0), ci))
    halo_next = pl.BlockSpec((None, DIL_SIDE, B_W),
                             lambda bi, ci, ui: (bi, jnp.minimum((ui + 1) * hb, nhb - 1), ci))

    o_spec = pl.BlockSpec((None, tu, B_W), lambda bi, ci, ui: (bi, ui, ci))
    lse_spec = pl.BlockSpec((None, tu, LSE_W), lambda bi, ci, ui: (bi, ui, ci))
    in_specs = [cur, halo_prev, cur, halo_next, halo_prev, cur, halo_next,
                pl.BlockSpec((B_HEADS, DIL_QB, DIL_KW), lambda bi, ci, ui: (0, 0, 0))]
    args = [qv, kv, kv, kv, vv, vv, vv, bias]
    if not first:
        in_specs += [o_spec, lse_spec]
        args += [prev[0].reshape(b, length, r * B_W), prev[1].reshape(b, length, r * LSE_W)]
    if last:
        out_specs = o_spec
        out_shape = jax.ShapeDtypeStruct((b, length, r * B_W), BF16)
    else:
        out_specs = [o_spec, lse_spec]
        out_shape = [jax.ShapeDtypeStruct((b, length, r * B_W), F32),
                     jax.ShapeDtypeStruct((b, length, r * LSE_W), F32)]
    res = pl.pallas_call(
        functools.partial(_dil_kernel, tu=tu, length=length, first=first, last=last),
        grid=(b, r, length // tu),
        in_specs=in_specs,
        out_specs=out_specs,
        out_shape=out_shape,
        scratch_shapes=[pltpu.VMEM((tu + 2 * DIL_SIDE, B_W), BF16),
                        pltpu.VMEM((tu + 2 * DIL_SIDE, B_W), BF16)],
        compiler_params=_cparams("parallel", "parallel", "parallel"),
        name=f"dil_g{g}",
    )(*args)
    if last:
        return res.reshape(b, s, B_W)
    return res[0].reshape(b, s, B_W), res[1].reshape(b, s, LSE_W)


def _memkv_kernel(m_ref, g_ref, w_ref, k_ref, v_ref):
    hm = _rms(m_ref[...], g_ref[...]).astype(BF16)
    kv = _dot(hm, w_ref[...])
    k_ref[...] = kv[:, :MEM_W].astype(BF16)
    v_ref[...] = kv[:, MEM_W:].astype(BF16)


def _memkv(mem, g, w):
    b, m, d = mem.shape
    spec = pl.BlockSpec((None, m, MEM_W), lambda bi: (bi, 0, 0))
    shape = jax.ShapeDtypeStruct((b, m, MEM_W), BF16)
    return pl.pallas_call(
        _memkv_kernel,
        grid=(b,),
        in_specs=[pl.BlockSpec((None, m, d), lambda bi: (bi, 0, 0)),
                  pl.BlockSpec((1, d), lambda bi: (0, 0)),
                  pl.BlockSpec((d, 2 * MEM_W), lambda bi: (0, 0))],
        out_specs=[spec, spec],
        out_shape=[shape, shape],
        compiler_params=_cparams("parallel"),
        name="memkv",
    )(mem, g, w)


def _mid_kernel(x_ref, a_ref, b_ref, wout_ref, g_ref, wq_ref, km_ref, vm_ref, wo_ref, o_ref):
    x2 = (x_ref[...] + _dot(a_ref[...], wout_ref[0:A_Q_W, :])
          + _dot(b_ref[...], wout_ref[A_Q_W:, :]))
    h = _rms(x2, g_ref[...]).astype(BF16)
    q = _dot(h, wq_ref[...]).astype(BF16)
    outs = []
    for hd in range(MEM_HEADS):
        cols = slice(hd * HEAD_DIM, (hd + 1) * HEAD_DIM)
        s = _dot_nt(q[:, cols], km_ref[:, cols]) * SCALE
        m = jnp.max(s, axis=-1, keepdims=True)
        p = jnp.exp(s - m)
        p = p / jnp.sum(p, axis=-1, keepdims=True)
        outs.append(_dot(p.astype(BF16), vm_ref[:, cols]).astype(BF16))
    o = jnp.concatenate(outs, axis=-1)
    o_ref[...] = x2 + _dot(o, wo_ref[...])


def _mid(x, a, bmix, wout, g, wq, km, vm, wo, *, tm):
    b, s, d = x.shape
    m = km.shape[1]
    const = lambda bi, si: (0, 0)
    return pl.pallas_call(
        _mid_kernel,
        grid=(b, s // tm),
        in_specs=[
            pl.BlockSpec((None, tm, d), lambda bi, si: (bi, si, 0)),
            pl.BlockSpec((None, tm, A_Q_W), lambda bi, si: (bi, si, 0)),
            pl.BlockSpec((None, tm, B_W), lambda bi, si: (bi, si, 0)),
            pl.BlockSpec((A_Q_W + B_W, d), const),
            pl.BlockSpec((1, d), const),
            pl.BlockSpec((d, MEM_W), const),
            pl.BlockSpec((None, m, MEM_W), lambda bi, si: (bi, 0, 0)),
            pl.BlockSpec((None, m, MEM_W), lambda bi, si: (bi, 0, 0)),
            pl.BlockSpec((MEM_W, d), const),
        ],
        out_specs=pl.BlockSpec((None, tm, d), lambda bi, si: (bi, si, 0)),
        out_shape=jax.ShapeDtypeStruct((b, s, d), F32),
        compiler_params=_cparams("parallel", "parallel"),
        name="mid",
    )(x, a, bmix, wout, g, wq, km, vm, wo)


def _rope_tables(seq_len):
    nf = HEAD_DIM // 4
    t = jnp.arange(seq_len)
    row = (t // GRID_W).astype(F32)
    col = (t % GRID_W).astype(F32)
    inv_freq = ROPE_THETA ** (-jnp.arange(nf, dtype=F32) / nf)
    ar = row[:, None] * inv_freq
    ac = col[:, None] * inv_freq
    cos = jnp.concatenate([jnp.cos(ar), jnp.cos(ar), jnp.cos(ac), jnp.cos(ac)], axis=-1)
    sin = jnp.concatenate([-jnp.sin(ar), jnp.sin(ar), -jnp.sin(ac), jnp.sin(ac)], axis=-1)
    return cos, sin


def _tile(n, pref):
    return pref if n % pref == 0 else n


def kernel(x, mem, ffn1_norm, ffn1_w_gate, ffn1_w_up, ffn1_w_down, mix_norm, w_in, q_norm_a, k_norm_a, rel_bias, w_out, mem_x_norm, mem_m_norm, w_q_mem, w_kv_mem, w_o_mem, ffn2_norm, ffn2_w_gate, ffn2_w_up, ffn2_w_down, final_norm):
    bsz, seq_len, d = x.shape
    depth = ffn1_norm.shape[0]
    t = bsz * seq_len
    tm = _tile(t, 512)
    ts = _tile(seq_len, 512)
    tf = _tile(ffn1_w_gate.shape[-1], 512)
    cos, sin = _rope_tables(seq_len)
    row = lambda v: v.reshape(1, -1).astype(F32)
    bf = lambda w: w.astype(BF16)

    xf = x.reshape(t, d)
    for l in range(depth):
        x1, hmix = _ffn(xf, row(ffn1_norm[l]), bf(ffn1_w_gate[l]), bf(ffn1_w_up[l]),
                        bf(ffn1_w_down[l]), row(mix_norm[l]), emit_h=True, tm=tm, tf=tf)
        w_in_l = bf(w_in[l])
        qa, ka, va = _proj_a(hmix.reshape(bsz, seq_len, d), w_in_l[:, :A_W], row(q_norm_a[l]),
                             row(k_norm_a[l]), cos, sin, tm=ts)
        out_a = _attn_a(qa, ka, va, tq=ts, tk=_tile(seq_len, 1024))
        state = None
        for g in reversed(range(N_DIL)):
            wcols = jnp.concatenate(
                [w_in_l[:, A_W + (part * N_DIL + g) * B_W:A_W + (part * N_DIL + g + 1) * B_W]
                 for part in range(3)], axis=1)
            qkv = _proj_b(hmix, wcols, DIL_RATES[g], tm=_tile(t, 1024))
            state = _dil_group(qkv, _band_bias(rel_bias, g), state, g, bsz, seq_len, last=(g == 0))
        out_b = state
        km, vm = _memkv(mem, row(mem_m_norm[l]), bf(w_kv_mem[l]))
        x3 = _mid(x1.reshape(bsz, seq_len, d), out_a, out_b, bf(w_out[l]), row(mem_x_norm[l]),
                  bf(w_q_mem[l]), km, vm, bf(w_o_mem[l]), tm=ts)
        x3 = x3.reshape(t, d)
        if l == depth - 1:
            xf = _ffn(x3, row(ffn2_norm[l]), bf(ffn2_w_gate[l]), bf(ffn2_w_up[l]),
                      bf(ffn2_w_down[l]), row(final_norm), emit_h=False, tm=tm, tf=tf)
        else:
            xf, _ = _ffn(x3, row(ffn2_norm[l]), bf(ffn2_w_gate[l]), bf(ffn2_w_up[l]),
                         bf(ffn2_w_down[l]), row(ffn1_norm[l + 1]), emit_h=True, tm=tm, tf=tf)
    return xf.reshape(bsz, seq_len, d)
```

```python
import functools
import math

import numpy as np
import jax
import jax.numpy as jnp
from jax import lax
from jax.experimental import pallas as pl
from jax.experimental.pallas import tpu as pltpu

F32 = jnp.float32
BF16 = jnp.bfloat16

HEAD_DIM = 128
A_Q_HEADS = 8
A_KV_HEADS = 2
A_GROUP = A_Q_HEADS // A_KV_HEADS
B_HEADS = 8
DIL_WINDOWS = (128, 512, 2048)
DIL_RATES = (1, 4, 16)
N_DIL = 3
MEM_HEADS = 4
NUM_BUCKETS = 32
MAX_DISTANCE = 1024
GRID_W = 64
ROPE_THETA = 10000.0
EPS = 1e-6
NEG = -1e30
SCALE = HEAD_DIM ** -0.5
LOG2E = math.log2(math.e)

A_Q_W = A_Q_HEADS * HEAD_DIM
A_KV_W = A_KV_HEADS * HEAD_DIM
B_W = B_HEADS * HEAD_DIM
A_W = A_Q_W + 2 * A_KV_W
MEM_W = MEM_HEADS * HEAD_DIM
DIL_SIDE = 64
DIL_QB = 128
DIL_KW = DIL_QB + 2 * DIL_SIDE
LSE_W = 128

LANES = 128
BF16_SUBLANES = 16
FFN_DOWN_CHUNKS = 4
PROJ_B_CHUNK = 256
VMEM_LIMIT = 56 * 1024 * 1024


def _cparams(*sem):
    return pltpu.CompilerParams(dimension_semantics=sem, vmem_limit_bytes=VMEM_LIMIT)


def _rms(x, g):
    return x * lax.rsqrt(jnp.mean(x * x, axis=-1, keepdims=True) + EPS) * g


def _dot(a, b):
    return jnp.dot(a, b, preferred_element_type=F32)


def _dot_nt(a, b):
    return lax.dot_general(a, b, (((1,), (1,)), ((), ())), preferred_element_type=F32)


def _ffn_kernel(x_ref, g_ref, wg_ref, wu_ref, wd_ref, g2_ref, *rest, emit_h):
    if emit_h:
        o_ref, h2_ref, h_scr, acc_scr = rest
    else:
        o_ref, h_scr, acc_scr = rest
    j = pl.program_id(1)

    @pl.when(j == 0)
    def _():
        h_scr[...] = _rms(x_ref[...], g_ref[...]).astype(BF16)
        acc_scr[...] = jnp.zeros(acc_scr.shape, F32)

    h = h_scr[...]
    gate = _dot(h, wg_ref[...])
    up = _dot(h, wu_ref[...])
    act = (gate * jax.nn.sigmoid(gate) * up).astype(BF16)
    cw = acc_scr.shape[1] // FFN_DOWN_CHUNKS
    for n in range(FFN_DOWN_CHUNKS):
        cols = slice(n * cw, (n + 1) * cw)
        acc_scr[:, cols] += _dot(act, wd_ref[:, cols])

    @pl.when(j == pl.num_programs(1) - 1)
    def _():
        y = x_ref[...] + 0.5 * acc_scr[...]
        if emit_h:
            o_ref[...] = y
            h2_ref[...] = _rms(y, g2_ref[...]).astype(BF16)
        else:
            o_ref[...] = _rms(y, g2_ref[...])


def _ffn(x, g, wg, wu, wd, g2, *, emit_h, tm, tf):
    t, d = x.shape
    f = wg.shape[1]
    out_shape = [jax.ShapeDtypeStruct((t, d), F32)]
    out_specs = [pl.BlockSpec((tm, d), lambda i, j: (i, 0))]
    if emit_h:
        out_shape.append(jax.ShapeDtypeStruct((t, d), BF16))
        out_specs.append(pl.BlockSpec((tm, d), lambda i, j: (i, 0)))
    res = pl.pallas_call(
        functools.partial(_ffn_kernel, emit_h=emit_h),
        grid=(t // tm, f // tf),
        in_specs=[
            pl.BlockSpec((tm, d), lambda i, j: (i, 0)),
            pl.BlockSpec((1, d), lambda i, j: (0, 0)),
            pl.BlockSpec((d, tf), lambda i, j: (0, j)),
            pl.BlockSpec((d, tf), lambda i, j: (0, j)),
            pl.BlockSpec((tf, d), lambda i, j: (j, 0)),
            pl.BlockSpec((1, d), lambda i, j: (0, 0)),
        ],
        out_specs=out_specs,
        out_shape=out_shape,
        scratch_shapes=[pltpu.VMEM((tm, d), BF16), pltpu.VMEM((tm, d), F32)],
        compiler_params=_cparams("parallel", "arbitrary"),
        name="ffn_h" if emit_h else "ffn_final",
    )(x, g, wg, wu, wd, g2)
    return res if emit_h else res[0]


def _proj_a_kernel(h_ref, w_ref, gq_ref, gk_ref, cos_ref, sin_ref, q_ref, k_ref, v_ref):
    acc = _dot(h_ref[...], w_ref[...])
    cos = cos_ref[...]
    sin = sin_ref[...]
    lane = lax.broadcasted_iota(jnp.int32, (1, HEAD_DIM), 1)
    first_half = (lane % (HEAD_DIM // 2)) < (HEAD_DIM // 4)

    def norm_rope(xh, g):
        xn = _rms(xh, g)
        swapped = jnp.where(first_half,
                            pltpu.roll(xn, HEAD_DIM - HEAD_DIM // 4, 1),
                            pltpu.roll(xn, HEAD_DIM // 4, 1))
        return xn * cos + swapped * sin

    gq = gq_ref[...]
    gk = gk_ref[...]
    for hd in range(A_Q_HEADS):
        cols = slice(hd * HEAD_DIM, (hd + 1) * HEAD_DIM)
        q_ref[hd] = (norm_rope(acc[:, cols], gq) * (SCALE * LOG2E)).T.astype(BF16)
    for hd in range(A_KV_HEADS):
        kc = slice(A_Q_W + hd * HEAD_DIM, A_Q_W + (hd + 1) * HEAD_DIM)
        vc = slice(A_Q_W + A_KV_W + hd * HEAD_DIM, A_Q_W + A_KV_W + (hd + 1) * HEAD_DIM)
        k_ref[hd] = norm_rope(acc[:, kc], gk).astype(BF16)
        v_ref[hd] = acc[:, vc].T.astype(BF16)


def _proj_a(h, w, gq, gk, cos, sin, *, tm):
    b, s, d = h.shape
    k_shape = jax.ShapeDtypeStruct((b, A_KV_HEADS, s, HEAD_DIM), BF16)
    k_spec = pl.BlockSpec((None, A_KV_HEADS, tm, HEAD_DIM), lambda bi, si: (bi, 0, si, 0))
    vt_shape = jax.ShapeDtypeStruct((b, A_KV_HEADS, HEAD_DIM, s), BF16)
    vt_spec = pl.BlockSpec((None, A_KV_HEADS, HEAD_DIM, tm), lambda bi, si: (bi, 0, 0, si))
    qt_shape = jax.ShapeDtypeStruct((b, A_Q_HEADS, HEAD_DIM, s), BF16)
    qt_spec = pl.BlockSpec((None, A_Q_HEADS, HEAD_DIM, tm), lambda bi, si: (bi, 0, 0, si))
    return pl.pallas_call(
        _proj_a_kernel,
        grid=(b, s // tm),
        in_specs=[
            pl.BlockSpec((None, tm, d), lambda bi, si: (bi, si, 0)),
            pl.BlockSpec((d, A_W), lambda bi, si: (0, 0)),
            pl.BlockSpec((1, HEAD_DIM), lambda bi, si: (0, 0)),
            pl.BlockSpec((1, HEAD_DIM), lambda bi, si: (0, 0)),
            pl.BlockSpec((tm, HEAD_DIM), lambda bi, si: (si, 0)),
            pl.BlockSpec((tm, HEAD_DIM), lambda bi, si: (si, 0)),
        ],
        out_specs=[qt_spec, k_spec, vt_spec],
        out_shape=[qt_shape, k_shape, vt_shape],
        compiler_params=_cparams("parallel", "parallel"),
        name="proj_a",
    )(h, w, gq, gk, cos, sin)


def _proj_b_kernel(h_ref, w_ref, o_ref, acc_scr, *, r):
    h = h_ref[...]
    rows = h.shape[0] // r
    for n in range(B_W // PROJ_B_CHUNK):
        acc = _dot(h, w_ref[:, n * PROJ_B_CHUNK:(n + 1) * PROJ_B_CHUNK])
        if r == 1:
            o_ref[:, n * PROJ_B_CHUNK:(n + 1) * PROJ_B_CHUNK] = acc.astype(BF16)
            continue
        for piece in range(PROJ_B_CHUNK // LANES):
            ch = n * (PROJ_B_CHUNK // LANES) + piece
            acc_scr[ch] = acc[:, piece * LANES:(piece + 1) * LANES]
            for c in range(r):
                col = c * B_W + ch * LANES
                o_ref[:, col:col + LANES] = acc_scr[ch, pl.ds(c, rows, stride=r), :].astype(BF16)


def _proj_b(h, w, r, *, tm):
    t, d = h.shape
    return pl.pallas_call(
        functools.partial(_proj_b_kernel, r=r),
        grid=(t // tm, 3),
        in_specs=[pl.BlockSpec((tm, d), lambda i, j: (i, 0)),
                  pl.BlockSpec((d, B_W), lambda i, j: (0, j))],
        out_specs=pl.BlockSpec((None, tm // r, r * B_W), lambda i, j: (j, i, 0)),
        out_shape=jax.ShapeDtypeStruct((3, t // r, r * B_W), BF16),
        scratch_shapes=[pltpu.VMEM((B_W // LANES, tm, LANES), F32)],
        compiler_params=_cparams("parallel", "parallel"),
        name=f"proj_b_r{r}",
    )(h, w)


def _attn_a_kernel(qt_ref, k_ref, vt_ref, o_ref, m_scr, l_scr, acc_scr, *, tk, nk):
    m_scr[...] = jnp.full(m_scr.shape, -jnp.inf, F32)
    l_scr[...] = jnp.zeros(l_scr.shape, F32)
    acc_scr[...] = jnp.zeros(acc_scr.shape, F32)

    def body(kb, carry):
        off = pl.multiple_of(kb * tk, tk)
        k = k_ref[pl.ds(off, tk), :]
        vt = vt_ref[:, pl.ds(off, tk)]
        vt = jnp.concatenate([vt, jnp.ones((BF16_SUBLANES, tk), BF16)], axis=0)

        def scores(hd):
            return _dot(k, qt_ref[hd])

        def softmax(hd, st):
            m_prev = m_scr[hd]
            m_new = jnp.maximum(m_prev, jnp.max(st, axis=0, keepdims=True))
            m_scr[hd] = m_new
            return jnp.exp2(st - m_new).astype(BF16), jnp.exp2(m_prev - m_new)

        def accumulate(hd, p, alpha):
            res = _dot(vt, p)
            acc_scr[hd] = alpha * acc_scr[hd] + res[:HEAD_DIM]
            l_scr[hd] = alpha * l_scr[hd] + res[HEAD_DIM:HEAD_DIM + 1]

        st, pa = {}, {}
        for i in range(A_GROUP + 2):
            if i < A_GROUP:
                st[i] = scores(i)
            if 0 <= i - 1 < A_GROUP:
                pa[i - 1] = softmax(i - 1, st.pop(i - 1))
            if 0 <= i - 2 < A_GROUP:
                accumulate(i - 2, *pa.pop(i - 2))
        return carry

    lax.fori_loop(0, nk, body, 0)
    for hd in range(A_GROUP):
        o = acc_scr[hd] / l_scr[hd]
        o_ref[:, hd * HEAD_DIM:(hd + 1) * HEAD_DIM] = o.T.astype(o_ref.dtype)


def _attn_a(qt, k, vt, *, tq, tk):
    b, _, _, s = qt.shape
    gw = A_GROUP * HEAD_DIM
    return pl.pallas_call(
        functools.partial(_attn_a_kernel, tk=tk, nk=s // tk),
        grid=(b, A_KV_HEADS, s // tq),
        in_specs=[
            pl.BlockSpec((None, A_GROUP, HEAD_DIM, tq), lambda bi, hi, qi: (bi, hi, 0, qi)),
            pl.BlockSpec((None, None, s, HEAD_DIM), lambda bi, hi, qi: (bi, hi, 0, 0)),
            pl.BlockSpec((None, None, HEAD_DIM, s), lambda bi, hi, qi: (bi, hi, 0, 0)),
        ],
        out_specs=pl.BlockSpec((None, tq, gw), lambda bi, hi, qi: (bi, qi, hi)),
        out_shape=jax.ShapeDtypeStruct((b, s, A_Q_W), BF16),
        scratch_shapes=[pltpu.VMEM((A_GROUP, 1, tq), F32), pltpu.VMEM((A_GROUP, 1, tq), F32),
                        pltpu.VMEM((A_GROUP, HEAD_DIM, tq), F32)],
        compiler_params=_cparams("parallel", "parallel", "arbitrary"),
        name="attn_a",
    )(qt, k, vt)


def _t5_buckets(rel):
    nb = NUM_BUCKETS // 2
    max_exact = nb // 2
    ret = (rel > 0).astype(np.int32) * nb
    n = np.abs(rel)
    large = max_exact + (np.log(np.maximum(n, 1) / max_exact)
                         / np.log(MAX_DISTANCE / max_exact) * (nb - max_exact)).astype(np.int32)
    large = np.minimum(large, nb - 1)
    return (ret + np.where(n < max_exact, n, large)).astype(np.int32)


def _band_bias(rel_bias, g):
    rel = DIL_RATES[g] * np.arange(-DIL_SIDE, DIL_SIDE + 1)
    buckets = _t5_buckets(rel)
    bias = rel_bias[:, g * B_HEADS:(g + 1) * B_HEADS].astype(F32)[buckets].T
    period = jnp.concatenate(
        [bias, jnp.full((B_HEADS, DIL_KW - 2 * DIL_SIDE), NEG, F32)], axis=1)
    flat = jnp.tile(period, (1, DIL_QB))[:, :DIL_QB * DIL_KW]
    return flat.reshape(B_HEADS, DIL_QB, DIL_KW)


def _dil_kernel(*refs, tu, length, first, last):
    q_ref, kp_ref, kc_ref, kn_ref, vp_ref, vc_ref, vn_ref, bias_ref = refs[:8]
    refs = refs[8:]
    if not first:
        op_ref, lp_ref = refs[:2]
        refs = refs[2:]
    if last:
        o_ref, kbuf, vbuf = refs
    else:
        o_ref, lse_ref, kbuf, vbuf = refs

    kbuf[0:DIL_SIDE] = kp_ref[...]
    kbuf[DIL_SIDE:DIL_SIDE + tu] = kc_ref[...]
    kbuf[DIL_SIDE + tu:] = kn_ref[...]
    vbuf[0:DIL_SIDE] = vp_ref[...]
    vbuf[DIL_SIDE:DIL_SIDE + tu] = vc_ref[...]
    vbuf[DIL_SIDE + tu:] = vn_ref[...]

    u0 = pl.program_id(2) * tu
    kk = lax.broadcasted_iota(jnp.int32, (DIL_QB, DIL_KW), 1)
    lane_h = lax.broadcasted_iota(jnp.int32, (DIL_QB, LSE_W), 1)
    for qi in range(tu // DIL_QB):
        rows = slice(qi * DIL_QB, (qi + 1) * DIL_QB)
        key_u = u0 + (qi * DIL_QB - DIL_SIDE) + kk
        in_seq = (key_u >= 0) & (key_u < length)
        if not first:
            lse_prev = lp_ref[rows, :]
        lse_out = jnp.zeros((DIL_QB, LSE_W), F32)
        for hd in range(B_HEADS):
            cols = slice(hd * HEAD_DIM, (hd + 1) * HEAD_DIM)
            q = q_ref[rows, cols]
            kw = kbuf[qi * DIL_QB:qi * DIL_QB + DIL_KW, cols]
            vw = vbuf[qi * DIL_QB:qi * DIL_QB + DIL_KW, cols]
            s = _dot_nt(q, kw) * SCALE
            s = jnp.where(in_seq, s + bias_ref[hd], NEG)
            m = jnp.max(s, axis=-1, keepdims=True)
            p = jnp.exp(s - m)
            l = jnp.sum(p, axis=-1, keepdims=True)
            o = _dot(p.astype(BF16), vw) / l
            lse = m + jnp.log(l)
            if not first:
                lp = lse_prev[:, hd:hd + 1]
                mx = jnp.maximum(lp, lse)
                ea = jnp.exp(lp - mx)
                eb = jnp.exp(lse - mx)
                den = ea + eb
                o = (op_ref[rows, cols] * ea + o * eb) / den
                lse = mx + jnp.log(den)
            o_ref[rows, cols] = o.astype(o_ref.dtype)
            if not last:
                lse_out = jnp.where(lane_h == hd, lse, lse_out)
        if not last:
            lse_ref[rows, :] = lse_out


def _dil_group(qkv, bias, prev, g, b, s, *, last):
    r = DIL_RATES[g]
    length = s // r
    tu = min(512, length)
    hb = tu // DIL_SIDE
    nhb = length // DIL_SIDE
    first = prev is None
    qkv = qkv.reshape(3, b, length, r * B_W)

    def cur(part):
        return pl.BlockSpec((None, None, tu, B_W), lambda bi, ci, ui: (part, bi, ui, ci))

    def halo_prev(part):
        return pl.BlockSpec((None, None, DIL_SIDE, B_W),
                            lambda bi, ci, ui: (part, bi, jnp.maximum(ui * hb - 1, 0), ci))

    def halo_next(part):
        return pl.BlockSpec((None, None, DIL_SIDE, B_W),
                            lambda bi, ci, ui: (part, bi, jnp.minimum((ui + 1) * hb, nhb - 1), ci))

    o_spec = pl.BlockSpec((None, tu, B_W), lambda bi, ci, ui: (bi, ui, ci))
    lse_spec = pl.BlockSpec((None, tu, LSE_W), lambda bi, ci, ui: (bi, ui, ci))
    in_specs = [cur(0), halo_prev(1), cur(1), halo_next(1), halo_prev(2), cur(2), halo_next(2),
                pl.BlockSpec((B_HEADS, DIL_QB, DIL_KW), lambda bi, ci, ui: (0, 0, 0))]
    args = [qkv] * 7 + [bias]
    if not first:
        in_specs += [o_spec, lse_spec]
        args += [prev[0].reshape(b, length, r * B_W), prev[1].reshape(b, length, r * LSE_W)]
    if last:
        out_specs = o_spec
        out_shape = jax.ShapeDtypeStruct((b, length, r * B_W), BF16)
    else:
        out_specs = [o_spec, lse_spec]
        out_shape = [jax.ShapeDtypeStruct((b, length, r * B_W), F32),
                     jax.ShapeDtypeStruct((b, length, r * LSE_W), F32)]
    res = pl.pallas_call(
        functools.partial(_dil_kernel, tu=tu, length=length, first=first, last=last),
        grid=(b, r, length // tu),
        in_specs=in_specs,
        out_specs=out_specs,
        out_shape=out_shape,
        scratch_shapes=[pltpu.VMEM((tu + 2 * DIL_SIDE, B_W), BF16),
                        pltpu.VMEM((tu + 2 * DIL_SIDE, B_W), BF16)],
        compiler_params=_cparams("parallel", "parallel", "parallel"),
        name=f"dil_g{g}",
    )(*args)
    if last:
        return res.reshape(b, s, B_W)
    return res[0].reshape(b, s, B_W), res[1].reshape(b, s, LSE_W)


def _memkv_kernel(m_ref, g_ref, w_ref, k_ref, v_ref):
    hm = _rms(m_ref[...], g_ref[...]).astype(BF16)
    kv = _dot(hm, w_ref[...])
    k_ref[...] = kv[:, :MEM_W].astype(BF16)
    v_ref[...] = kv[:, MEM_W:].astype(BF16)


def _memkv(mem, g, w):
    b, m, d = mem.shape
    spec = pl.BlockSpec((None, m, MEM_W), lambda bi: (bi, 0, 0))
    shape = jax.ShapeDtypeStruct((b, m, MEM_W), BF16)
    return pl.pallas_call(
        _memkv_kernel,
        grid=(b,),
        in_specs=[pl.BlockSpec((None, m, d), lambda bi: (bi, 0, 0)),
                  pl.BlockSpec((1, d), lambda bi: (0, 0)),
                  pl.BlockSpec((d, 2 * MEM_W), lambda bi: (0, 0))],
        out_specs=[spec, spec],
        out_shape=[shape, shape],
        compiler_params=_cparams("parallel"),
        name="memkv",
    )(mem, g, w)


def _mid_kernel(x_ref, a_ref, b_ref, wout_ref, g_ref, wq_ref, km_ref, vm_ref, wo_ref, o_ref):
    x2 = (x_ref[...] + _dot(a_ref[...], wout_ref[0:A_Q_W, :])
          + _dot(b_ref[...], wout_ref[A_Q_W:, :]))
    h = _rms(x2, g_ref[...]).astype(BF16)
    q = _dot(h, wq_ref[...]).astype(BF16)
    outs = []
    for hd in range(MEM_HEADS):
        cols = slice(hd * HEAD_DIM, (hd + 1) * HEAD_DIM)
        s = _dot_nt(q[:, cols], km_ref[:, cols]) * SCALE
        m = jnp.max(s, axis=-1, keepdims=True)
        p = jnp.exp(s - m)
        p = p / jnp.sum(p, axis=-1, keepdims=True)
        outs.append(_dot(p.astype(BF16), vm_ref[:, cols]).astype(BF16))
    o = jnp.concatenate(outs, axis=-1)
    o_ref[...] = x2 + _dot(o, wo_ref[...])


def _mid(x, a, bmix, wout, g, wq, km, vm, wo, *, tm):
    b, s, d = x.shape
    m = km.shape[1]
    const = lambda bi, si: (0, 0)
    return pl.pallas_call(
        _mid_kernel,
        grid=(b, s // tm),
        in_specs=[
            pl.BlockSpec((None, tm, d), lambda bi, si: (bi, si, 0)),
            pl.BlockSpec((None, tm, A_Q_W), lambda bi, si: (bi, si, 0)),
            pl.BlockSpec((None, tm, B_W), lambda bi, si: (bi, si, 0)),
            pl.BlockSpec((A_Q_W + B_W, d), const),
            pl.BlockSpec((1, d), const),
            pl.BlockSpec((d, MEM_W), const),
            pl.BlockSpec((None, m, MEM_W), lambda bi, si: (bi, 0, 0)),
            pl.BlockSpec((None, m, MEM_W), lambda bi, si: (bi, 0, 0)),
            pl.BlockSpec((MEM_W, d), const),
        ],
        out_specs=pl.BlockSpec((None, tm, d), lambda bi, si: (bi, si, 0)),
        out_shape=jax.ShapeDtypeStruct((b, s, d), F32),
        compiler_params=_cparams("parallel", "parallel"),
        name="mid",
    )(x, a, bmix, wout, g, wq, km, vm, wo)


def _rope_tables(seq_len):
    nf = HEAD_DIM // 4
    t = jnp.arange(seq_len)
    row = (t // GRID_W).astype(F32)
    col = (t % GRID_W).astype(F32)
    inv_freq = ROPE_THETA ** (-jnp.arange(nf, dtype=F32) / nf)
    ar = row[:, None] * inv_freq
    ac = col[:, None] * inv_freq
    cos = jnp.concatenate([jnp.cos(ar), jnp.cos(ar), jnp.cos(ac), jnp.cos(ac)], axis=-1)
    sin = jnp.concatenate([-jnp.sin(ar), jnp.sin(ar), -jnp.sin(ac), jnp.sin(ac)], axis=-1)
    return cos, sin


def _tile(n, pref):
    return pref if n % pref == 0 else n


def kernel(x, mem, ffn1_norm, ffn1_w_gate, ffn1_w_up, ffn1_w_down, mix_norm, w_in, q_norm_a, k_norm_a, rel_bias, w_out, mem_x_norm, mem_m_norm, w_q_mem, w_kv_mem, w_o_mem, ffn2_norm, ffn2_w_gate, ffn2_w_up, ffn2_w_down, final_norm):
    bsz, seq_len, d = x.shape
    depth = ffn1_norm.shape[0]
    t = bsz * seq_len
    tm = _tile(t, 512)
    ts = _tile(seq_len, 512)
    tf = _tile(ffn1_w_gate.shape[-1], 512)
    cos, sin = _rope_tables(seq_len)
    row = lambda v: v.reshape(1, -1).astype(F32)
    bf = lambda w: w.astype(BF16)

    xf = x.reshape(t, d)
    for l in range(depth):
        x1, hmix = _ffn(xf, row(ffn1_norm[l]), bf(ffn1_w_gate[l]), bf(ffn1_w_up[l]),
                        bf(ffn1_w_down[l]), row(mix_norm[l]), emit_h=True, tm=tm, tf=tf)
        w_in_l = bf(w_in[l])
        qa, ka, va = _proj_a(hmix.reshape(bsz, seq_len, d), w_in_l[:, :A_W], row(q_norm_a[l]),
                             row(k_norm_a[l]), cos, sin, tm=ts)
        out_a = _attn_a(qa, ka, va, tq=ts, tk=_tile(seq_len, 2048))
        state = None
        for g in reversed(range(N_DIL)):
            wcols = jnp.concatenate(
                [w_in_l[:, A_W + (part * N_DIL + g) * B_W:A_W + (part * N_DIL + g + 1) * B_W]
                 for part in range(3)], axis=1)
            qkv = _proj_b(hmix, wcols, DIL_RATES[g], tm=_tile(t, 1024))
            state = _dil_group(qkv, _band_bias(rel_bias, g), state, g, bsz, seq_len, last=(g == 0))
        out_b = state
        km, vm = _memkv(mem, row(mem_m_norm[l]), bf(w_kv_mem[l]))
        x3 = _mid(x1.reshape(bsz, seq_len, d), out_a, out_b, bf(w_out[l]), row(mem_x_norm[l]),
                  bf(w_q_mem[l]), km, vm, bf(w_o_mem[l]), tm=ts)
        x3 = x3.reshape(t, d)
        if l == depth - 1:
            xf = _ffn(x3, row(ffn2_norm[l]), bf(ffn2_w_gate[l]), bf(ffn2_w_up[l]),
                      bf(ffn2_w_down[l]), row(final_norm), emit_h=False, tm=tm, tf=tf)
        else:
            xf, _ = _ffn(x3, row(ffn2_norm[l]), bf(ffn2_w_gate[l]), bf(ffn2_w_up[l]),
                         bf(ffn2_w_down[l]), row(ffn1_norm[l + 1]), emit_h=True, tm=tm, tf=tf)
    return xf.reshape(bsz, seq_len, d)
```
